```python
import jax, jax.numpy as jnp
from jax import lax
import numpy as np

D_MODEL = 2048
BATCH = 4
SEQ = 2048
DEPTH = 4

CHUNK = 64
Q_BLOCK = 128
N_MIXERS = 3
N_MLA_LAYERS = (DEPTH + 2) // 3
N_SB_LAYERS = (DEPTH + 1) // 3
N_CA_LAYERS = DEPTH // 3

MLA_HEADS = 16
MLA_Q_LORA = 512
MLA_KV_LORA = 512
MLA_NOPE = 128
MLA_ROPE = 64
MLA_V = 128
ROPE_THETA = 10000.0

SB_HEADS = 16
SB_HEAD_DIM = 128

CA_HEADS = 16
CA_HEAD_DIM = 128
CA_LEFT_CHUNKS = 8
REL_CLIP_LEFT = 128
REL_TABLE = REL_CLIP_LEFT + CHUNK

D_FF = 4 * D_MODEL

LN_EPS = 1e-5
RMS_EPS = 1e-6
DEEPNORM_ALPHA = (2.0 * DEPTH) ** 0.25
DEEPNORM_BETA = (8.0 * DEPTH) ** -0.25
NEG = -1e30

kernel_name = "hybrid_streaming_mla_stickbreak_chunkrel_deepnorm"


def layer_norm(x, g, b):
    xf = x.astype(jnp.float32)
    mu = jnp.mean(xf, -1, keepdims=True)
    var = jnp.mean(jnp.square(xf - mu), -1, keepdims=True)
    return ((xf - mu) * lax.rsqrt(var + LN_EPS) * g.astype(jnp.float32) + b.astype(jnp.float32)).astype(x.dtype)


def rms_norm(x, g):
    xf = x.astype(jnp.float32)
    return (xf * lax.rsqrt(jnp.mean(xf * xf, -1, keepdims=True) + RMS_EPS) * g.astype(jnp.float32)).astype(x.dtype)


def rope(x, pos):
    half = x.shape[-1] // 2
    inv = ROPE_THETA ** (-jnp.arange(half, dtype=jnp.float32) / half)
    ang = pos.astype(jnp.float32)[:, None] * inv[None, :]
    cos = jnp.cos(ang)[None, :, None, :]
    sin = jnp.sin(ang)[None, :, None, :]
    x1 = x[..., :half].astype(jnp.float32)
    x2 = x[..., half:].astype(jnp.float32)
    return jnp.concatenate([x1 * cos - x2 * sin, x2 * cos + x1 * sin], -1).astype(x.dtype)


def mla_mixer(x, w_down, q_norm_g, w_uq, kv_norm_g, w_ukv, w_o):
    B, S, _ = x.shape
    H = MLA_HEADS
    down = x @ w_down
    c_q, c_kv, k_rope = jnp.split(down, [MLA_Q_LORA, MLA_Q_LORA + MLA_KV_LORA], axis=-1)
    q = (rms_norm(c_q, q_norm_g) @ w_uq).reshape(B, S, H, MLA_NOPE + MLA_ROPE)
    kv = (rms_norm(c_kv, kv_norm_g) @ w_ukv).reshape(B, S, H, MLA_NOPE + MLA_V)
    q_nope, q_rope = q[..., :MLA_NOPE], q[..., MLA_NOPE:]
    k_nope, v = kv[..., :MLA_NOPE], kv[..., MLA_NOPE:]
    pos = jnp.arange(S)
    q_rope = rope(q_rope, pos)
    k_rope = rope(k_rope[:, :, None, :], pos)[:, :, 0, :]
    scale = (MLA_NOPE + MLA_ROPE) ** -0.5
    chunk_id = pos // CHUNK
    outs = []
    for qs in range(0, S, Q_BLOCK):
        ke = qs + Q_BLOCK
        s = (jnp.einsum('bqhd,bkhd->bhqk', q_nope[:, qs:ke], k_nope[:, :ke])
             + jnp.einsum('bqhr,bkr->bhqk', q_rope[:, qs:ke], k_rope[:, :ke])).astype(jnp.float32) * scale
        mask = chunk_id[None, :ke] <= chunk_id[qs:ke, None]
        p = jax.nn.softmax(jnp.where(mask, s, NEG), axis=-1).astype(v.dtype)
        outs.append(jnp.einsum('bhqk,bkhd->bqhd', p, v[:, :ke]))
    o = jnp.concatenate(outs, axis=1).reshape(B, S, H * MLA_V)
    return o @ w_o


def stick_breaking_mixer(x, w_qkv, w_o):
    B, S, _ = x.shape
    H, Dh = SB_HEADS, SB_HEAD_DIM
    q, k, v = jnp.split((x @ w_qkv).reshape(B, S, 3, H, Dh), 3, axis=2)
    q, k, v = q[:, :, 0], k[:, :, 0], v[:, :, 0]
    scale = Dh ** -0.5
    pos = jnp.arange(S)
    outs = []
    for qs in range(0, S, Q_BLOCK):
        ke = qs + Q_BLOCK
        z = jnp.einsum('bqhd,bkhd->bhqk', q[:, qs:ke], k[:, :ke]).astype(jnp.float32) * scale
        strict = pos[None, :ke] < pos[qs:ke, None]
        log_beta = jax.nn.log_sigmoid(z)
        log_1m = jnp.where(strict, jax.nn.log_sigmoid(-z), 0.0)
        log_surv = lax.cumsum(log_1m, axis=3, reverse=True) - log_1m
        a = jnp.where(strict, jnp.exp(log_beta + log_surv), 0.0).astype(v.dtype)
        outs.append(jnp.einsum('bhqk,bkhd->bqhd', a, v[:, :ke]))
    o = jnp.concatenate(outs, axis=1).reshape(B, S, H * Dh)
    return o @ w_o


def chunked_relpos_mixer(x, w_qkv, rel_bias, w_o):
    B, S, _ = x.shape
    H, Dh = CA_HEADS, CA_HEAD_DIM
    n_chunks = S // CHUNK
    pad = CA_LEFT_CHUNKS * CHUNK
    band = pad + CHUNK
    q, k, v = jnp.split((x @ w_qkv).reshape(B, S, 3, H, Dh), 3, axis=2)
    q, k, v = q[:, :, 0], k[:, :, 0], v[:, :, 0]
    qc = q.reshape(B, n_chunks, CHUNK, H, Dh)
    k_pad = jnp.pad(k, ((0, 0), (pad, 0), (0, 0), (0, 0)))
    v_pad = jnp.pad(v, ((0, 0), (pad, 0), (0, 0), (0, 0)))
    key_idx = jnp.arange(n_chunks)[:, None] * CHUNK + jnp.arange(band)[None, :]
    kb = k_pad[:, key_idx]
    vb = v_pad[:, key_idx]
    s = jnp.einsum('bcqhd,bckhd->bhcqk', qc, kb).astype(jnp.float32) * (Dh ** -0.5)
    rel = jnp.arange(band)[None, :] - pad - jnp.arange(CHUNK)[:, None]
    bias = rel_bias[jnp.clip(rel, -REL_CLIP_LEFT, CHUNK - 1) + REL_CLIP_LEFT]
    s = s + jnp.transpose(bias, (2, 0, 1)).astype(jnp.float32)[None, :, None]
    valid = (key_idx - pad) >= 0
    s = jnp.where(valid[None, None, :, None, :], s, NEG)
    p = jax.nn.softmax(s, axis=-1).astype(vb.dtype)
    o = jnp.einsum('bhcqk,bckhd->bcqhd', p, vb).reshape(B, S, H * Dh)
    return o @ w_o


def sq_relu_mlp(x, w_in, w_out):
    return jnp.square(jax.nn.relu(x @ w_in)) @ w_out


def setup_inputs(seed: int = 0) -> dict:
    key = jax.random.key(seed)
    keys = iter(jax.random.split(key, 32))

    def nrm(shape, scale):
        return jax.random.normal(next(keys), shape, jnp.float32) * scale

    x = nrm((BATCH, SEQ, D_MODEL), 1.0)
    ln_mix_g = 1.0 + nrm((DEPTH, D_MODEL), 0.02)
    ln_mix_b = nrm((DEPTH, D_MODEL), 0.02)
    ln_ffn_g = 1.0 + nrm((DEPTH, D_MODEL), 0.02)
    ln_ffn_b = nrm((DEPTH, D_MODEL), 0.02)
    ffn_w_in = nrm((DEPTH, D_MODEL, D_FF), D_MODEL ** -0.5 * DEEPNORM_BETA)
    ffn_w_out = nrm((DEPTH, D_FF, D_MODEL), D_FF ** -0.5 * DEEPNORM_BETA)

    nA = N_MLA_LAYERS
    mla_w_down = nrm((nA, D_MODEL, MLA_Q_LORA + MLA_KV_LORA + MLA_ROPE), D_MODEL ** -0.5)
    mla_q_norm_g = 1.0 + nrm((nA, MLA_Q_LORA), 0.02)
    mla_w_uq = nrm((nA, MLA_Q_LORA, MLA_HEADS * (MLA_NOPE + MLA_ROPE)), MLA_Q_LORA ** -0.5)
    mla_kv_norm_g = 1.0 + nrm((nA, MLA_KV_LORA), 0.02)
    w_uk = nrm((nA, MLA_KV_LORA, MLA_HEADS, MLA_NOPE), MLA_KV_LORA ** -0.5)
    w_uv = nrm((nA, MLA_KV_LORA, MLA_HEADS, MLA_V), MLA_KV_LORA ** -0.5 * DEEPNORM_BETA)
    mla_w_ukv = jnp.concatenate([w_uk, w_uv], -1).reshape(nA, MLA_KV_LORA, MLA_HEADS * (MLA_NOPE + MLA_V))
    mla_w_o = nrm((nA, MLA_HEADS * MLA_V, D_MODEL), (MLA_HEADS * MLA_V) ** -0.5 * DEEPNORM_BETA)

    nB = N_SB_LAYERS
    sb_qk = nrm((nB, D_MODEL, 2, SB_HEADS * SB_HEAD_DIM), D_MODEL ** -0.5)
    sb_v = nrm((nB, D_MODEL, 1, SB_HEADS * SB_HEAD_DIM), D_MODEL ** -0.5 * DEEPNORM_BETA)
    sb_w_qkv = jnp.concatenate([sb_qk, sb_v], 2).reshape(nB, D_MODEL, 3 * SB_HEADS * SB_HEAD_DIM)
    sb_w_o = nrm((nB, SB_HEADS * SB_HEAD_DIM, D_MODEL), (SB_HEADS * SB_HEAD_DIM) ** -0.5 * DEEPNORM_BETA)

    nC = N_CA_LAYERS
    ca_qk = nrm((nC, D_MODEL, 2, CA_HEADS * CA_HEAD_DIM), D_MODEL ** -0.5)
    ca_v = nrm((nC, D_MODEL, 1, CA_HEADS * CA_HEAD_DIM), D_MODEL ** -0.5 * DEEPNORM_BETA)
    ca_w_qkv = jnp.concatenate([ca_qk, ca_v], 2).reshape(nC, D_MODEL, 3 * CA_HEADS * CA_HEAD_DIM)
    ca_rel_bias = nrm((nC, REL_TABLE, CA_HEADS), 0.5)
    ca_w_o = nrm((nC, CA_HEADS * CA_HEAD_DIM, D_MODEL), (CA_HEADS * CA_HEAD_DIM) ** -0.5 * DEEPNORM_BETA)

    return {"x": x, "ln_mix_g": ln_mix_g, "ln_mix_b": ln_mix_b, "ln_ffn_g": ln_ffn_g, "ln_ffn_b": ln_ffn_b,
            "ffn_w_in": ffn_w_in, "ffn_w_out": ffn_w_out,
            "mla_w_down": mla_w_down, "mla_q_norm_g": mla_q_norm_g, "mla_w_uq": mla_w_uq,
            "mla_kv_norm_g": mla_kv_norm_g, "mla_w_ukv": mla_w_ukv, "mla_w_o": mla_w_o,
            "sb_w_qkv": sb_w_qkv, "sb_w_o": sb_w_o,
            "ca_w_qkv": ca_w_qkv, "ca_rel_bias": ca_rel_bias, "ca_w_o": ca_w_o}


def reference(x, ln_mix_g, ln_mix_b, ln_ffn_g, ln_ffn_b, ffn_w_in, ffn_w_out,
              mla_w_down, mla_q_norm_g, mla_w_uq, mla_kv_norm_g, mla_w_ukv, mla_w_o,
              sb_w_qkv, sb_w_o, ca_w_qkv, ca_rel_bias, ca_w_o):
    h = x
    for i in range(DEPTH):
        kind = i % N_MIXERS
        slot = i // N_MIXERS
        if kind == 0:
            m = mla_mixer(h, mla_w_down[slot], mla_q_norm_g[slot], mla_w_uq[slot],
                          mla_kv_norm_g[slot], mla_w_ukv[slot], mla_w_o[slot])
        elif kind == 1:
            m = stick_breaking_mixer(h, sb_w_qkv[slot], sb_w_o[slot])
        else:
            m = chunked_relpos_mixer(h, ca_w_qkv[slot], ca_rel_bias[slot], ca_w_o[slot])
        h = layer_norm(DEEPNORM_ALPHA * h + m, ln_mix_g[i], ln_mix_b[i])
        h = layer_norm(DEEPNORM_ALPHA * h + sq_relu_mlp(h, ffn_w_in[i], ffn_w_out[i]),
                       ln_ffn_g[i], ln_ffn_b[i])
    return h
```

```python
import functools

import numpy as np
import jax
import jax.numpy as jnp
from jax import lax
from jax.experimental import pallas as pl
from jax.experimental.pallas import tpu as pltpu

BF16 = jnp.bfloat16
F32 = jnp.float32

DEPTH = 4
CHUNK = 64
MLA_HEADS = 16
MLA_Q_LORA = 512
MLA_KV_LORA = 512
MLA_NOPE = 128
MLA_ROPE = 64
MLA_V = 128
ROPE_THETA = 10000.0
SB_HEADS = 16
SB_HEAD_DIM = 128
CA_HEADS = 16
CA_HEAD_DIM = 128
CA_LEFT_CHUNKS = 8
REL_CLIP_LEFT = 128
REL_TABLE = REL_CLIP_LEFT + CHUNK
LN_EPS = 1e-5
RMS_EPS = 1e-6
DEEPNORM_ALPHA = (2.0 * DEPTH) ** 0.25
NEG = -1e30

LANES = 128
VMEM_LIMIT_BYTES = 56 * 1024 * 1024

ATT_TQ = 128
ATT_TK = 128
CA_PAD = CA_LEFT_CHUNKS * CHUNK
CA_BAND = CA_PAD + ATT_TQ


def _params(*sem):
    return pltpu.CompilerParams(dimension_semantics=sem, vmem_limit_bytes=VMEM_LIMIT_BYTES)


def _layer_norm(y, g, b):
    mu = jnp.mean(y, axis=-1, keepdims=True)
    d = y - mu
    var = jnp.mean(d * d, axis=-1, keepdims=True)
    return d * lax.rsqrt(var + LN_EPS) * g + b


def _rms_norm(x, g):
    return x * lax.rsqrt(jnp.mean(x * x, axis=-1, keepdims=True) + RMS_EPS) * g


def _dot(a, b):
    return jnp.dot(a, b, preferred_element_type=F32)


def _dot_nt(a, b):
    return lax.dot_general(a, b, (((1,), (1,)), ((), ())), preferred_element_type=F32)


def _ffn_kernel(x_ref, win_ref, wout_ref, g_ref, b_ref, o_ref, xb_ref, *, n_ff_steps, ln_rows):
    j = pl.program_id(1)

    @pl.when(j == 0)
    def _():
        xb_ref[...] = x_ref[...].astype(BF16)
        o_ref[...] = jnp.zeros_like(o_ref)

    hid = _dot(xb_ref[...], win_ref[...])
    hid = jnp.square(jnp.maximum(hid, 0.0)).astype(BF16)
    o_ref[...] += _dot(hid, wout_ref[...])

    @pl.when(j == n_ff_steps - 1)
    def _():
        g = g_ref[...]
        b = b_ref[...]

        def body(r, carry):
            rows = pl.ds(pl.multiple_of(r * ln_rows, ln_rows), ln_rows)
            y = DEEPNORM_ALPHA * x_ref[rows, :] + o_ref[rows, :]
            o_ref[rows, :] = _layer_norm(y, g, b)
            return carry

        lax.fori_loop(0, o_ref.shape[0] // ln_rows, body, 0)


def _ffn(x, w_in, w_out, g, b, *, bm=1024, bf=512, ln_rows=128):
    t, d = x.shape
    f = w_in.shape[1]
    bm = min(bm, t)
    n_ff_steps = f // bf
    return pl.pallas_call(
        functools.partial(_ffn_kernel, n_ff_steps=n_ff_steps, ln_rows=ln_rows),
        grid=(t // bm, n_ff_steps),
        in_specs=[
            pl.BlockSpec((bm, d), lambda i, j: (i, 0)),
            pl.BlockSpec((d, bf), lambda i, j: (0, j)),
            pl.BlockSpec((bf, d), lambda i, j: (j, 0)),
            pl.BlockSpec((1, d), lambda i, j: (0, 0)),
            pl.BlockSpec((1, d), lambda i, j: (0, 0)),
        ],
        out_specs=pl.BlockSpec((bm, d), lambda i, j: (i, 0)),
        out_shape=jax.ShapeDtypeStruct((t, d), F32),
        scratch_shapes=[pltpu.VMEM((bm, d), BF16)],
        compiler_params=_params("parallel", "arbitrary"),
        name="ffn_ln",
    )(x, w_in, w_out, g, b)


def _proj_ln_kernel(a_ref, w_ref, h_ref, g_ref, b_ref, o_ref):
    y = DEEPNORM_ALPHA * h_ref[...] + _dot(a_ref[...], w_ref[...])
    o_ref[...] = _layer_norm(y, g_ref[...], b_ref[...])


def _proj_ln(a, w, h, g, b, *, bm=256):
    t, d = h.shape
    k = a.shape[1]
    bm = min(bm, t)
    return pl.pallas_call(
        _proj_ln_kernel,
        grid=(t // bm,),
        in_specs=[
            pl.BlockSpec((bm, k), lambda i: (i, 0)),
            pl.BlockSpec((k, d), lambda i: (0, 0)),
            pl.BlockSpec((bm, d), lambda i: (i, 0)),
            pl.BlockSpec((1, d), lambda i: (0, 0)),
            pl.BlockSpec((1, d), lambda i: (0, 0)),
        ],
        out_specs=pl.BlockSpec((bm, d), lambda i: (i, 0)),
        out_shape=jax.ShapeDtypeStruct((t, d), F32),
        compiler_params=_params("parallel"),
        name="proj_ln",
    )(a, w, h, g, b)


def _qkv_kernel(x_ref, w_ref, o_ref, xb_ref):
    @pl.when(pl.program_id(1) == 0)
    def _():
        xb_ref[...] = x_ref[...].astype(BF16)

    o_ref[...] = _dot(xb_ref[...], w_ref[...]).astype(o_ref.dtype)


def _qkv_proj(x, w, *, bm=1024, bn=1024):
    t, d = x.shape
    n = w.shape[1]
    bm = min(bm, t)
    return pl.pallas_call(
        _qkv_kernel,
        grid=(t // bm, n // bn),
        in_specs=[
            pl.BlockSpec((bm, d), lambda i, j: (i, 0)),
            pl.BlockSpec((d, bn), lambda i, j: (0, j)),
        ],
        out_specs=pl.BlockSpec((bm, bn), lambda i, j: (i, j)),
        out_shape=jax.ShapeDtypeStruct((t, n), BF16),
        scratch_shapes=[pltpu.VMEM((bm, d), BF16)],
        compiler_params=_params("parallel", "arbitrary"),
        name="qkv_proj",
    )(x, w)


def _rope_lanes(x, cos_t, sin_a, sin_b):
    half = MLA_ROPE // 2
    return (x * cos_t + pltpu.roll(x, half, 1) * sin_a + pltpu.roll(x, LANES - half, 1) * sin_b)


def _mla_proj_kernel(h_ref, wd_ref, gq_ref, gkv_ref, wuq_ref, wukv_ref, cos_ref, sa_ref, sb_ref,
                     q_ref, k_ref, v_ref):
    cos_t = cos_ref[...]
    sin_a = sa_ref[...]
    sin_b = sb_ref[...]
    down = _dot(h_ref[...].astype(BF16), wd_ref[...])
    c_q = _rms_norm(down[:, :MLA_Q_LORA], gq_ref[...]).astype(BF16)
    c_kv = _rms_norm(down[:, MLA_Q_LORA:MLA_Q_LORA + MLA_KV_LORA], gkv_ref[...]).astype(BF16)
    k_rope = _rope_lanes(down[:, MLA_Q_LORA + MLA_KV_LORA:], cos_t, sin_a, sin_b).astype(BF16)
    wq = 2 * LANES
    for hd in range(MLA_HEADS):
        qh = _dot(c_q, wuq_ref[:, hd * wq:(hd + 1) * wq])
        q_ref[:, hd * wq:hd * wq + LANES] = qh[:, :LANES].astype(BF16)
        q_ref[:, hd * wq + LANES:(hd + 1) * wq] = _rope_lanes(qh[:, LANES:], cos_t, sin_a, sin_b).astype(BF16)
        kvh = _dot(c_kv, wukv_ref[:, hd * wq:(hd + 1) * wq])
        k_ref[:, hd * wq:hd * wq + LANES] = kvh[:, :LANES].astype(BF16)
        k_ref[:, hd * wq + LANES:(hd + 1) * wq] = k_rope
        v_ref[:, hd * LANES:(hd + 1) * LANES] = kvh[:, LANES:].astype(BF16)


def _mla_proj(h, wd, gq, gkv, wuq, wukv, cos_t, sin_a, sin_b, *, seq, bm=256):
    t, d = h.shape
    bm = min(bm, seq)
    nd = wd.shape[1]
    nq = wuq.shape[1]
    nkv = wukv.shape[1]
    pos_blocks = seq // bm
    full = lambda i: (0, 0)
    pos = lambda i: (i % pos_blocks, 0)
    row = lambda i: (i, 0)
    return pl.pallas_call(
        _mla_proj_kernel,
        grid=(t // bm,),
        in_specs=[
            pl.BlockSpec((bm, d), row),
            pl.BlockSpec((d, nd), full),
            pl.BlockSpec((1, MLA_Q_LORA), full),
            pl.BlockSpec((1, MLA_KV_LORA), full),
            pl.BlockSpec((MLA_Q_LORA, nq), full),
            pl.BlockSpec((MLA_KV_LORA, nkv), full),
            pl.BlockSpec((bm, LANES), pos),
            pl.BlockSpec((bm, LANES), pos),
            pl.BlockSpec((bm, LANES), pos),
        ],
        out_specs=[
            pl.BlockSpec((bm, nq), row),
            pl.BlockSpec((bm, nq), row),
            pl.BlockSpec((bm, MLA_HEADS * MLA_V), row),
        ],
        out_shape=[
            jax.ShapeDtypeStruct((t, nq), BF16),
            jax.ShapeDtypeStruct((t, nq), BF16),
            jax.ShapeDtypeStruct((t, MLA_HEADS * MLA_V), BF16),
        ],
        compiler_params=_params("parallel"),
        name="mla_proj",
    )(h, wd, gq, gkv, wuq, wukv, cos_t, sin_a, sin_b)


def _rope_tables(seq):
    half = MLA_ROPE // 2
    inv = ROPE_THETA ** (-jnp.arange(half, dtype=F32) / half)
    ang = jnp.arange(seq, dtype=F32)[:, None] * inv[None, :]
    cos = jnp.cos(ang)
    sin = jnp.sin(ang)
    zero = jnp.zeros_like(cos)
    pad = jnp.zeros((seq, LANES - MLA_ROPE), F32)
    cos_t = jnp.concatenate([cos, cos, pad], axis=1)
    sin_a = jnp.concatenate([zero, sin, pad], axis=1)
    sin_b = jnp.concatenate([-sin, zero, pad], axis=1)
    return cos_t, sin_a, sin_b


def _mla_attn_kernel(q_ref, k_ref, v_ref, o_ref, *, scale):
    i = pl.program_id(2)
    q = q_ref[...]

    def block(j, carry, mask):
        m, l, acc = carry
        rows = pl.ds(pl.multiple_of(j * ATT_TK, ATT_TK), ATT_TK)
        s = _dot_nt(q, k_ref[rows, :]) * scale
        if mask is not None:
            s = jnp.where(mask, s, NEG)
        m_new = jnp.maximum(m, jnp.max(s, axis=-1, keepdims=True))
        p = jnp.exp(s - m_new)
        corr = jnp.exp(m - m_new)
        l = corr * l + jnp.sum(p, axis=-1, keepdims=True)
        acc = corr * acc + _dot(p.astype(BF16), v_ref[rows, :])
        return m_new, l, acc

    init = (jnp.full((ATT_TQ, 1), NEG, F32), jnp.zeros((ATT_TQ, 1), F32),
            jnp.zeros((ATT_TQ, MLA_V), F32))
    carry = lax.fori_loop(0, i, lambda j, c: block(j, c, None), init)
    qc = lax.broadcasted_iota(jnp.int32, (ATT_TQ, ATT_TK), 0) // CHUNK
    kc = lax.broadcasted_iota(jnp.int32, (ATT_TQ, ATT_TK), 1) // CHUNK
    m, l, acc = block(i, carry, kc <= qc)
    o_ref[...] = (acc / l).astype(o_ref.dtype)


def _mla_attn(q, kcat, v, *, batch, seq):
    nq = seq // ATT_TQ
    wq = 2 * LANES
    scale = (MLA_NOPE + MLA_ROPE) ** -0.5
    return pl.pallas_call(
        functools.partial(_mla_attn_kernel, scale=scale),
        grid=(batch, MLA_HEADS, nq),
        in_specs=[
            pl.BlockSpec((ATT_TQ, wq), lambda b, h, i: (b * nq + i, h)),
            pl.BlockSpec((seq, wq), lambda b, h, i: (b, h)),
            pl.BlockSpec((seq, MLA_V), lambda b, h, i: (b, h)),
        ],
        out_specs=pl.BlockSpec((ATT_TQ, MLA_V), lambda b, h, i: (b * nq + i, h)),
        out_shape=jax.ShapeDtypeStruct((batch * seq, MLA_HEADS * MLA_V), BF16),
        compiler_params=_params("parallel", "parallel", "arbitrary"),
        name="mla_attn",
    )(q, kcat, v)


def _sb_attn_kernel(q_ref, k_ref, v_ref, tri_ref, o_ref, *, scale):
    i = pl.program_id(2)
    q = q_ref[...]
    tri = tri_ref[...]

    def block(j, carry, strict):
        run, acc = carry
        rows = pl.ds(pl.multiple_of(j * ATT_TK, ATT_TK), ATT_TK)
        z = _dot_nt(q, k_ref[rows, :]) * scale
        soft = jnp.log1p(jnp.exp(-jnp.abs(z)))
        log_beta = jnp.minimum(z, 0.0) - soft
        log_1m = jnp.minimum(-z, 0.0) - soft
        if strict is not None:
            log_1m = jnp.where(strict, log_1m, 0.0)
        hi = log_1m.astype(BF16)
        lo = (log_1m - hi.astype(F32)).astype(BF16)
        inner = _dot(jnp.concatenate([hi, lo], axis=1), tri)
        a = jnp.exp(log_beta + inner + run)
        if strict is not None:
            a = jnp.where(strict, a, 0.0)
        acc = acc + _dot(a.astype(BF16), v_ref[rows, :])
        run = run + jnp.sum(log_1m, axis=-1, keepdims=True)
        return run, acc

    row = lax.broadcasted_iota(jnp.int32, (ATT_TQ, ATT_TK), 0)
    col = lax.broadcasted_iota(jnp.int32, (ATT_TQ, ATT_TK), 1)
    init = (jnp.zeros((ATT_TQ, 1), F32), jnp.zeros((ATT_TQ, SB_HEAD_DIM), F32))
    carry = block(i, init, col < row)
    run, acc = lax.fori_loop(0, i, lambda t, c: block(i - 1 - t, c, None), carry)
    o_ref[...] = acc.astype(o_ref.dtype)


def _sb_attn(qkv, *, batch, seq):
    nq = seq // ATT_TQ
    hd = SB_HEADS
    scale = SB_HEAD_DIM ** -0.5
    idx = np.arange(ATT_TK)
    tri = (idx[:, None] > idx[None, :]).astype(np.float32)
    tri = jnp.asarray(np.concatenate([tri, tri], axis=0), BF16)
    return pl.pallas_call(
        functools.partial(_sb_attn_kernel, scale=scale),
        grid=(batch, hd, nq),
        in_specs=[
            pl.BlockSpec((ATT_TQ, SB_HEAD_DIM), lambda b, h, i: (b * nq + i, h)),
            pl.BlockSpec((seq, SB_HEAD_DIM), lambda b, h, i: (b, hd + h)),
            pl.BlockSpec((seq, SB_HEAD_DIM), lambda b, h, i: (b, 2 * hd + h)),
            pl.BlockSpec((2 * ATT_TK, ATT_TK), lambda b, h, i: (0, 0)),
        ],
        out_specs=pl.BlockSpec((ATT_TQ, SB_HEAD_DIM), lambda b, h, i: (b * nq + i, h)),
        out_shape=jax.ShapeDtypeStruct((batch * seq, hd * SB_HEAD_DIM), BF16),
        compiler_params=_params("parallel", "parallel", "arbitrary"),
        name="sb_attn",
    )(qkv, qkv, qkv, tri)


def _ca_bias_kernel(tab_ref, bias_ref, mask_ref):
    h = pl.program_id(0)
    qo = lax.broadcasted_iota(jnp.int32, (ATT_TQ, CA_BAND), 0)
    kb = lax.broadcasted_iota(jnp.int32, (ATT_TQ, CA_BAND), 1)
    idx = jnp.clip(kb - CA_PAD - qo, -REL_CLIP_LEFT, CHUNK - 1) + REL_CLIP_LEFT
    acc = jnp.zeros((ATT_TQ, CA_BAND), F32)
    for r in range(REL_TABLE):
        acc = jnp.where(idx == r, tab_ref[h, r], acc)
    bias_ref[0] = acc
    qc = qo // CHUNK
    kc = kb // CHUNK
    inside = jnp.where(kc >= qc, jnp.where(kc <= qc + CA_LEFT_CHUNKS, 1.0, 0.0), 0.0)
    mask_ref[0] = inside.astype(F32)


def _ca_bias(rel_bias):
    tab = rel_bias.T.astype(F32)
    return pl.pallas_call(
        _ca_bias_kernel,
        grid=(CA_HEADS,),
        in_specs=[pl.BlockSpec(memory_space=pltpu.SMEM)],
        out_specs=[
            pl.BlockSpec((1, ATT_TQ, CA_BAND), lambda h: (h, 0, 0)),
            pl.BlockSpec((1, ATT_TQ, CA_BAND), lambda h: (h, 0, 0)),
        ],
        out_shape=[
            jax.ShapeDtypeStruct((CA_HEADS, ATT_TQ, CA_BAND), F32),
            jax.ShapeDtypeStruct((CA_HEADS, ATT_TQ, CA_BAND), F32),
        ],
        compiler_params=_params("arbitrary"),
        name="ca_bias",
    )(tab)


def _ca_attn_kernel(q_ref, k_ref, v_ref, bias_ref, mask_ref, o_ref, kpad_ref, vpad_ref, *, scale):
    i = pl.program_id(2)

    @pl.when(i == 0)
    def _():
        kpad_ref[:CA_PAD, :] = jnp.zeros((CA_PAD, CA_HEAD_DIM), BF16)
        vpad_ref[:CA_PAD, :] = jnp.zeros((CA_PAD, CA_HEAD_DIM), BF16)
        kpad_ref[CA_PAD:, :] = k_ref[...]
        vpad_ref[CA_PAD:, :] = v_ref[...]

    rows = pl.ds(pl.multiple_of(i * ATT_TQ, ATT_TQ), CA_BAND)
    s = _dot_nt(q_ref[...], kpad_ref[rows, :]) * scale + bias_ref[0]
    kb = lax.broadcasted_iota(jnp.int32, (ATT_TQ, CA_BAND), 1)
    inside = jnp.where(kb >= CA_PAD - i * ATT_TQ, mask_ref[0], 0.0)
    s = jnp.where(inside > 0.5, s, NEG)
    m = jnp.max(s, axis=-1, keepdims=True)
    p = jnp.exp(s - m)
    l = jnp.sum(p, axis=-1, keepdims=True)
    o = _dot(p.astype(BF16), vpad_ref[rows, :])
    o_ref[...] = (o / l).astype(o_ref.dtype)


def _ca_attn(qkv, bias, mask, *, batch, seq):
    nq = seq // ATT_TQ
    hd = CA_HEADS
    scale = CA_HEAD_DIM ** -0.5
    return pl.pallas_call(
        functools.partial(_ca_attn_kernel, scale=scale),
        grid=(batch, hd, nq),
        in_specs=[
            pl.BlockSpec((ATT_TQ, CA_HEAD_DIM), lambda b, h, i: (b * nq + i, h)),
            pl.BlockSpec((seq, CA_HEAD_DIM), lambda b, h, i: (b, hd + h)),
            pl.BlockSpec((seq, CA_HEAD_DIM), lambda b, h, i: (b, 2 * hd + h)),
            pl.BlockSpec((1, ATT_TQ, CA_BAND), lambda b, h, i: (h, 0, 0)),
            pl.BlockSpec((1, ATT_TQ, CA_BAND), lambda b, h, i: (h, 0, 0)),
        ],
        out_specs=pl.BlockSpec((ATT_TQ, CA_HEAD_DIM), lambda b, h, i: (b * nq + i, h)),
        out_shape=jax.ShapeDtypeStruct((batch * seq, hd * CA_HEAD_DIM), BF16),
        scratch_shapes=[pltpu.VMEM((seq + CA_PAD, CA_HEAD_DIM), BF16),
                        pltpu.VMEM((seq + CA_PAD, CA_HEAD_DIM), BF16)],
        compiler_params=_params("parallel", "parallel", "arbitrary"),
        name="ca_attn",
    )(qkv, qkv, qkv, bias, mask)


def _mla_mixer(h, w_down, q_norm_g, w_uq, kv_norm_g, w_ukv, *, batch, seq):
    d = h.shape[1]
    n_down = w_down.shape[1]
    n_down_pad = -(-n_down // LANES) * LANES
    wd = jnp.pad(w_down, ((0, 0), (0, n_down_pad - n_down))).astype(BF16)
    wq = 2 * LANES
    wuq = w_uq.reshape(MLA_Q_LORA, MLA_HEADS, MLA_NOPE + MLA_ROPE)
    wuq = jnp.pad(wuq, ((0, 0), (0, 0), (0, wq - MLA_NOPE - MLA_ROPE)))
    wuq = wuq.reshape(MLA_Q_LORA, MLA_HEADS * wq).astype(BF16)
    cos_t, sin_a, sin_b = _rope_tables(seq)
    q, kcat, v = _mla_proj(h, wd, q_norm_g[None, :], kv_norm_g[None, :], wuq, w_ukv.astype(BF16),
                           cos_t, sin_a, sin_b, seq=seq)
    return _mla_attn(q, kcat, v, batch=batch, seq=seq)


def kernel(x, ln_mix_g, ln_mix_b, ln_ffn_g, ln_ffn_b, ffn_w_in, ffn_w_out, mla_w_down, mla_q_norm_g,
           mla_w_uq, mla_kv_norm_g, mla_w_ukv, mla_w_o, sb_w_qkv, sb_w_o, ca_w_qkv, ca_rel_bias, ca_w_o):
    batch, seq, d = x.shape
    h = x.reshape(batch * seq, d)
    for i in range(DEPTH):
        kind = i % 3
        slot = i // 3
        if kind == 0:
            a = _mla_mixer(h, mla_w_down[slot], mla_q_norm_g[slot], mla_w_uq[slot],
                           mla_kv_norm_g[slot], mla_w_ukv[slot], batch=batch, seq=seq)
            w_o = mla_w_o[slot]
        elif kind == 1:
            qkv = _qkv_proj(h, sb_w_qkv[slot].astype(BF16))
            a = _sb_attn(qkv, batch=batch, seq=seq)
            w_o = sb_w_o[slot]
        else:
            qkv = _qkv_proj(h, ca_w_qkv[slot].astype(BF16))
            bias, mask = _ca_bias(ca_rel_bias[slot])
            a = _ca_attn(qkv, bias, mask, batch=batch, seq=seq)
            w_o = ca_w_o[slot]
        h = _proj_ln(a, w_o.astype(BF16), h, ln_mix_g[i][None, :], ln_mix_b[i][None, :])
        h = _ffn(h, ffn_w_in[i].astype(BF16), ffn_w_out[i].astype(BF16),
                 ln_ffn_g[i][None, :], ln_ffn_b[i][None, :])
    return h.reshape(batch, seq, d)
```

```python
import functools

import numpy as np
import jax
import jax.numpy as jnp
from jax import lax
from jax.experimental import pallas as pl
from jax.experimental.pallas import tpu as pltpu

BF16 = jnp.bfloat16
F32 = jnp.float32

DEPTH = 4
CHUNK = 64
MLA_HEADS = 16
MLA_Q_LORA = 512
MLA_KV_LORA = 512
MLA_NOPE = 128
MLA_ROPE = 64
MLA_V = 128
ROPE_THETA = 10000.0
SB_HEADS = 16
SB_HEAD_DIM = 128
CA_HEADS = 16
CA_HEAD_DIM = 128
CA_LEFT_CHUNKS = 8
REL_CLIP_LEFT = 128
REL_TABLE = REL_CLIP_LEFT + CHUNK
LN_EPS = 1e-5
RMS_EPS = 1e-6
DEEPNORM_ALPHA = (2.0 * DEPTH) ** 0.25
NEG = -1e30

LANES = 128
VMEM_LIMIT_BYTES = 56 * 1024 * 1024

ATT_TQ = 256
CA_PAD = CA_LEFT_CHUNKS * CHUNK
CA_BAND = CA_PAD + ATT_TQ
CA_TABLE_LANES = 1024


def _params(*sem):
    return pltpu.CompilerParams(dimension_semantics=sem, vmem_limit_bytes=VMEM_LIMIT_BYTES)


def _layer_norm(y, g, b):
    mu = jnp.mean(y, axis=-1, keepdims=True)
    d = y - mu
    var = jnp.mean(d * d, axis=-1, keepdims=True)
    return d * lax.rsqrt(var + LN_EPS) * g + b


def _rms_norm(x, g):
    return x * lax.rsqrt(jnp.mean(x * x, axis=-1, keepdims=True) + RMS_EPS) * g


def _dot(a, b):
    return jnp.dot(a, b, preferred_element_type=F32)


def _dot_nt(a, b):
    return lax.dot_general(a, b, (((1,), (1,)), ((), ())), preferred_element_type=F32)


def _ffn_kernel(x_ref, win_ref, wout_ref, g_ref, b_ref, o_ref, xb_ref, *, n_ff_steps, ln_rows):
    j = pl.program_id(1)

    @pl.when(j == 0)
    def _():
        xb_ref[...] = x_ref[...].astype(BF16)
        o_ref[...] = jnp.zeros_like(o_ref)

    hid = _dot(xb_ref[...], win_ref[...])
    hid = jnp.square(jnp.maximum(hid, 0.0)).astype(BF16)
    o_ref[...] += _dot(hid, wout_ref[...])

    @pl.when(j == n_ff_steps - 1)
    def _():
        g = g_ref[...]
        b = b_ref[...]

        def body(r, carry):
            rows = pl.ds(pl.multiple_of(r * ln_rows, ln_rows), ln_rows)
            y = DEEPNORM_ALPHA * x_ref[rows, :] + o_ref[rows, :]
            o_ref[rows, :] = _layer_norm(y, g, b)
            return carry

        lax.fori_loop(0, o_ref.shape[0] // ln_rows, body, 0)


def _ffn(x, w_in, w_out, g, b, *, bm=1024, bf=512, ln_rows=128):
    t, d = x.shape
    f = w_in.shape[1]
    bm = min(bm, t)
    n_ff_steps = f // bf
    return pl.pallas_call(
        functools.partial(_ffn_kernel, n_ff_steps=n_ff_steps, ln_rows=ln_rows),
        grid=(t // bm, n_ff_steps),
        in_specs=[
            pl.BlockSpec((bm, d), lambda i, j: (i, 0)),
            pl.BlockSpec((d, bf), lambda i, j: (0, j)),
            pl.BlockSpec((bf, d), lambda i, j: (j, 0)),
            pl.BlockSpec((1, d), lambda i, j: (0, 0)),
            pl.BlockSpec((1, d), lambda i, j: (0, 0)),
        ],
        out_specs=pl.BlockSpec((bm, d), lambda i, j: (i, 0)),
        out_shape=jax.ShapeDtypeStruct((t, d), F32),
        scratch_shapes=[pltpu.VMEM((bm, d), BF16)],
        compiler_params=_params("parallel", "arbitrary"),
        name="ffn_ln",
    )(x, w_in, w_out, g, b)


def _proj_ln_kernel(a_ref, w_ref, h_ref, g_ref, b_ref, o_ref):
    y = DEEPNORM_ALPHA * h_ref[...] + _dot(a_ref[...], w_ref[...])
    o_ref[...] = _layer_norm(y, g_ref[...], b_ref[...])


def _proj_ln(a, w, h, g, b, *, bm=256):
    t, d = h.shape
    k = a.shape[1]
    bm = min(bm, t)
    return pl.pallas_call(
        _proj_ln_kernel,
        grid=(t // bm,),
        in_specs=[
            pl.BlockSpec((bm, k), lambda i: (i, 0)),
            pl.BlockSpec((k, d), lambda i: (0, 0)),
            pl.BlockSpec((bm, d), lambda i: (i, 0)),
            pl.BlockSpec((1, d), lambda i: (0, 0)),
            pl.BlockSpec((1, d), lambda i: (0, 0)),
        ],
        out_specs=pl.BlockSpec((bm, d), lambda i: (i, 0)),
        out_shape=jax.ShapeDtypeStruct((t, d), F32),
        compiler_params=_params("parallel"),
        name="proj_ln",
    )(a, w, h, g, b)


def _qkv_kernel(x_ref, w_ref, o_ref, xb_ref):
    @pl.when(pl.program_id(1) == 0)
    def _():
        xb_ref[...] = x_ref[...].astype(BF16)

    o_ref[...] = _dot(xb_ref[...], w_ref[...]).astype(o_ref.dtype)


def _qkv_proj(x, w, *, bm=1024, bn=1024):
    t, d = x.shape
    n = w.shape[1]
    bm = min(bm, t)
    return pl.pallas_call(
        _qkv_kernel,
        grid=(t // bm, n // bn),
        in_specs=[
            pl.BlockSpec((bm, d), lambda i, j: (i, 0)),
            pl.BlockSpec((d, bn), lambda i, j: (0, j)),
        ],
        out_specs=pl.BlockSpec((bm, bn), lambda i, j: (i, j)),
        out_shape=jax.ShapeDtypeStruct((t, n), BF16),
        scratch_shapes=[pltpu.VMEM((bm, d), BF16)],
        compiler_params=_params("parallel", "arbitrary"),
        name="qkv_proj",
    )(x, w)


def _rope_lanes(x, cos_t, sin_a, sin_b):
    half = MLA_ROPE // 2
    return (x * cos_t + pltpu.roll(x, half, 1) * sin_a + pltpu.roll(x, LANES - half, 1) * sin_b)


def _mla_proj_kernel(h_ref, wd_ref, gq_ref, gkv_ref, wuq_ref, wukv_ref, cos_ref, sa_ref, sb_ref,
                     q_ref, k_ref, v_ref):
    cos_t = cos_ref[...]
    sin_a = sa_ref[...]
    sin_b = sb_ref[...]
    down = _dot(h_ref[...].astype(BF16), wd_ref[...])
    c_q = _rms_norm(down[:, :MLA_Q_LORA], gq_ref[...]).astype(BF16)
    c_kv = _rms_norm(down[:, MLA_Q_LORA:MLA_Q_LORA + MLA_KV_LORA], gkv_ref[...]).astype(BF16)
    k_rope = _rope_lanes(down[:, MLA_Q_LORA + MLA_KV_LORA:], cos_t, sin_a, sin_b).astype(BF16)
    wq = 2 * LANES
    for hd in range(MLA_HEADS):
        qh = _dot(c_q, wuq_ref[:, hd * wq:(hd + 1) * wq])
        q_ref[:, hd * wq:hd * wq + LANES] = qh[:, :LANES].astype(BF16)
        q_ref[:, hd * wq + LANES:(hd + 1) * wq] = _rope_lanes(qh[:, LANES:], cos_t, sin_a, sin_b).astype(BF16)
        kvh = _dot(c_kv, wukv_ref[:, hd * wq:(hd + 1) * wq])
        k_ref[:, hd * wq:hd * wq + LANES] = kvh[:, :LANES].astype(BF16)
        k_ref[:, hd * wq + LANES:(hd + 1) * wq] = k_rope
        v_ref[:, hd * LANES:(hd + 1) * LANES] = kvh[:, LANES:].astype(BF16)


def _mla_proj(h, wd, gq, gkv, wuq, wukv, cos_t, sin_a, sin_b, *, seq, bm=256):
    t, d = h.shape
    bm = min(bm, seq)
    nd = wd.shape[1]
    nq = wuq.shape[1]
    nkv = wukv.shape[1]
    pos_blocks = seq // bm
    full = lambda i: (0, 0)
    pos = lambda i: (i % pos_blocks, 0)
    row = lambda i: (i, 0)
    return pl.pallas_call(
        _mla_proj_kernel,
        grid=(t // bm,),
        in_specs=[
            pl.BlockSpec((bm, d), row),
            pl.BlockSpec((d, nd), full),
            pl.BlockSpec((1, MLA_Q_LORA), full),
            pl.BlockSpec((1, MLA_KV_LORA), full),
            pl.BlockSpec((MLA_Q_LORA, nq), full),
            pl.BlockSpec((MLA_KV_LORA, nkv), full),
            pl.BlockSpec((bm, LANES), pos),
            pl.BlockSpec((bm, LANES), pos),
            pl.BlockSpec((bm, LANES), pos),
        ],
        out_specs=[
            pl.BlockSpec((bm, nq), row),
            pl.BlockSpec((bm, nq), row),
            pl.BlockSpec((bm, MLA_HEADS * MLA_V), row),
        ],
        out_shape=[
            jax.ShapeDtypeStruct((t, nq), BF16),
            jax.ShapeDtypeStruct((t, nq), BF16),
            jax.ShapeDtypeStruct((t, MLA_HEADS * MLA_V), BF16),
        ],
        compiler_params=_params("parallel"),
        name="mla_proj",
    )(h, wd, gq, gkv, wuq, wukv, cos_t, sin_a, sin_b)


def _rope_tables(seq):
    half = MLA_ROPE // 2
    inv = ROPE_THETA ** (-jnp.arange(half, dtype=F32) / half)
    ang = jnp.arange(seq, dtype=F32)[:, None] * inv[None, :]
    cos = jnp.cos(ang)
    sin = jnp.sin(ang)
    zero = jnp.zeros_like(cos)
    pad = jnp.zeros((seq, LANES - MLA_ROPE), F32)
    cos_t = jnp.concatenate([cos, cos, pad], axis=1)
    sin_a = jnp.concatenate([zero, sin, pad], axis=1)
    sin_b = jnp.concatenate([-sin, zero, pad], axis=1)
    return cos_t, sin_a, sin_b


def _mla_attn_kernel(q_ref, k_ref, v_ref, o_ref, *, scale, seq):
    qc = lax.broadcasted_iota(jnp.int32, (ATT_TQ, ATT_TQ), 0) // CHUNK
    kc = lax.broadcasted_iota(jnp.int32, (ATT_TQ, ATT_TQ), 1) // CHUNK
    visible = kc <= qc
    for t in range(seq // ATT_TQ):
        lo = t * ATT_TQ
        q = q_ref[lo:lo + ATT_TQ, :]
        s_diag = jnp.where(visible, _dot_nt(q, k_ref[lo:lo + ATT_TQ, :]) * scale, NEG)
        m = jnp.max(s_diag, axis=-1, keepdims=True)
        if t > 0:
            s_left = _dot_nt(q, k_ref[:lo, :]) * scale
            m = jnp.maximum(m, jnp.max(s_left, axis=-1, keepdims=True))
        p_diag = jnp.exp(s_diag - m)
        l = jnp.sum(p_diag, axis=-1, keepdims=True)
        o = _dot(p_diag.astype(BF16), v_ref[lo:lo + ATT_TQ, :])
        if t > 0:
            p_left = jnp.exp(s_left - m)
            l = l + jnp.sum(p_left, axis=-1, keepdims=True)
            o = o + _dot(p_left.astype(BF16), v_ref[:lo, :])
        o_ref[lo:lo + ATT_TQ, :] = (o * (1.0 / l)).astype(o_ref.dtype)


def _mla_attn(q, kcat, v, *, batch, seq):
    wq = 2 * LANES
    scale = (MLA_NOPE + MLA_ROPE) ** -0.5
    return pl.pallas_call(
        functools.partial(_mla_attn_kernel, scale=scale, seq=seq),
        grid=(batch, MLA_HEADS),
        in_specs=[
            pl.BlockSpec((seq, wq), lambda b, h: (b, h)),
            pl.BlockSpec((seq, wq), lambda b, h: (b, h)),
            pl.BlockSpec((seq, MLA_V), lambda b, h: (b, h)),
        ],
        out_specs=pl.BlockSpec((seq, MLA_V), lambda b, h: (b, h)),
        out_shape=jax.ShapeDtypeStruct((batch * seq, MLA_HEADS * MLA_V), BF16),
        compiler_params=_params("parallel", "parallel"),
        name="mla_attn",
    )(q, kcat, v)


def _sb_attn_kernel(q_ref, k_ref, v_ref, tri_ref, o_ref, *, scale, seq):
    row = lax.broadcasted_iota(jnp.int32, (ATT_TQ, ATT_TQ), 0)
    col = lax.broadcasted_iota(jnp.int32, (ATT_TQ, ATT_TQ), 1)
    strict = col < row
    tri = tri_ref[...]
    for t in range(seq // ATT_TQ):
        lo = t * ATT_TQ
        q = q_ref[lo:lo + ATT_TQ, :]
        run = None
        acc = None
        for kb in range(t, -1, -1):
            ks = kb * ATT_TQ
            z = _dot_nt(q, k_ref[ks:ks + ATT_TQ, :]) * scale
            soft = jnp.log1p(jnp.exp(-jnp.abs(z)))
            log_beta = jnp.minimum(z, 0.0) - soft
            log_1m = jnp.minimum(-z, 0.0) - soft
            if kb == t:
                log_1m = jnp.where(strict, log_1m, 0.0)
            hi = log_1m.astype(BF16)
            lo_part = (log_1m - hi.astype(F32)).astype(BF16)
            log_surv = _dot(jnp.concatenate([hi, lo_part], axis=1), tri)
            if run is not None:
                log_surv = log_surv + run
            a = jnp.exp(log_beta + log_surv)
            if kb == t:
                a = jnp.where(strict, a, 0.0)
            pv = _dot(a.astype(BF16), v_ref[ks:ks + ATT_TQ, :])
            acc = pv if acc is None else acc + pv
            total = jnp.sum(log_1m, axis=-1, keepdims=True)
            run = total if run is None else run + total
        o_ref[lo:lo + ATT_TQ, :] = acc.astype(o_ref.dtype)


def _sb_attn(qkv, *, batch, seq):
    hd = SB_HEADS
    scale = SB_HEAD_DIM ** -0.5
    idx = np.arange(ATT_TQ)
    tri = (idx[:, None] > idx[None, :]).astype(np.float32)
    tri = jnp.asarray(np.concatenate([tri, tri], axis=0), BF16)
    return pl.pallas_call(
        functools.partial(_sb_attn_kernel, scale=scale, seq=seq),
        grid=(batch, hd),
        in_specs=[
            pl.BlockSpec((seq, SB_HEAD_DIM), lambda b, h: (b, h)),
            pl.BlockSpec((seq, SB_HEAD_DIM), lambda b, h: (b, hd + h)),
            pl.BlockSpec((seq, SB_HEAD_DIM), lambda b, h: (b, 2 * hd + h)),
            pl.BlockSpec((2 * ATT_TQ, ATT_TQ), lambda b, h: (0, 0)),
        ],
        out_specs=pl.BlockSpec((seq, SB_HEAD_DIM), lambda b, h: (b, h)),
        out_shape=jax.ShapeDtypeStruct((batch * seq, hd * SB_HEAD_DIM), BF16),
        compiler_params=_params("parallel", "parallel"),
        name="sb_attn",
    )(qkv, qkv, qkv, tri)


def _ca_bias_kernel(row_ref, bias_ref):
    rows = jnp.broadcast_to(row_ref[0], (ATT_TQ, CA_TABLE_LANES))
    toeplitz = pltpu.roll(rows, 0, 1, stride=1, stride_axis=0)
    bias_ref[0] = toeplitz[:, :CA_BAND]


def _ca_bias(rel_bias):
    tab = rel_bias.T.astype(F32)
    left = jnp.broadcast_to(tab[:, :1], (CA_HEADS, CA_PAD - REL_CLIP_LEFT))
    right = jnp.broadcast_to(tab[:, -1:], (CA_HEADS, CA_BAND - CA_PAD - CHUNK))
    wrap = jnp.broadcast_to(tab[:, :1], (CA_HEADS, CA_TABLE_LANES - CA_BAND))
    row = jnp.concatenate([left, tab, right, wrap], axis=1)[:, None, :]
    return pl.pallas_call(
        _ca_bias_kernel,
        grid=(CA_HEADS,),
        in_specs=[pl.BlockSpec((1, 1, CA_TABLE_LANES), lambda h: (h, 0, 0))],
        out_specs=pl.BlockSpec((1, ATT_TQ, CA_BAND), lambda h: (h, 0, 0)),
        out_shape=jax.ShapeDtypeStruct((CA_HEADS, ATT_TQ, CA_BAND), F32),
        compiler_params=_params("parallel"),
        name="ca_bias",
    )(row)


def _ca_band_mask():
    qc = np.arange(ATT_TQ)[:, None] // CHUNK + CA_LEFT_CHUNKS
    kc = np.arange(CA_BAND)[None, :] // CHUNK
    return jnp.asarray(((kc <= qc) & (kc >= qc - CA_LEFT_CHUNKS)).astype(np.float32))


def _ca_attn_kernel(q_ref, k_ref, v_ref, bias_ref, mask_ref, o_ref, *, scale, seq):
    for t in range(seq // ATT_TQ):
        lo = t * ATT_TQ
        ks = max(0, lo - CA_PAD)
        c0 = ks - (lo - CA_PAD)
        s = _dot_nt(q_ref[lo:lo + ATT_TQ, :], k_ref[ks:lo + ATT_TQ, :]) * scale + bias_ref[0, :, c0:]
        s = jnp.where(mask_ref[:, c0:] > 0.5, s, NEG)
        m = jnp.max(s, axis=-1, keepdims=True)
        p = jnp.exp(s - m)
        l = jnp.sum(p, axis=-1, keepdims=True)
        o = _dot(p.astype(BF16), v_ref[ks:lo + ATT_TQ, :])
        o_ref[lo:lo + ATT_TQ, :] = (o * (1.0 / l)).astype(o_ref.dtype)


def _ca_attn(qkv, bias, mask, *, batch, seq):
    hd = CA_HEADS
    scale = CA_HEAD_DIM ** -0.5
    return pl.pallas_call(
        functools.partial(_ca_attn_kernel, scale=scale, seq=seq),
        grid=(batch, hd),
        in_specs=[
            pl.BlockSpec((seq, CA_HEAD_DIM), lambda b, h: (b, h)),
            pl.BlockSpec((seq, CA_HEAD_DIM), lambda b, h: (b, hd + h)),
            pl.BlockSpec((seq, CA_HEAD_DIM), lambda b, h: (b, 2 * hd + h)),
            pl.BlockSpec((1, ATT_TQ, CA_BAND), lambda b, h: (h, 0, 0)),
            pl.BlockSpec((ATT_TQ, CA_BAND), lambda b, h: (0, 0)),
        ],
        out_specs=pl.BlockSpec((seq, CA_HEAD_DIM), lambda b, h: (b, h)),
        out_shape=jax.ShapeDtypeStruct((batch * seq, hd * CA_HEAD_DIM), BF16),
        compiler_params=_params("parallel", "parallel"),
        name="ca_attn",
    )(qkv, qkv, qkv, bias, mask)


def _mla_mixer(h, w_down, q_norm_g, w_uq, kv_norm_g, w_ukv, *, batch, seq):
    n_down = w_down.shape[1]
    n_down_pad = -(-n_down // LANES) * LANES
    wd = jnp.pad(w_down, ((0, 0), (0, n_down_pad - n_down))).astype(BF16)
    wq = 2 * LANES
    wuq = w_uq.reshape(MLA_Q_LORA, MLA_HEADS, MLA_NOPE + MLA_ROPE)
    wuq = jnp.pad(wuq, ((0, 0), (0, 0), (0, wq - MLA_NOPE - MLA_ROPE)))
    wuq = wuq.reshape(MLA_Q_LORA, MLA_HEADS * wq).astype(BF16)
    cos_t, sin_a, sin_b = _rope_tables(seq)
    q, kcat, v = _mla_proj(h, wd, q_norm_g[None, :], kv_norm_g[None, :], wuq, w_ukv.astype(BF16),
                           cos_t, sin_a, sin_b, seq=seq)
    return _mla_attn(q, kcat, v, batch=batch, seq=seq)


def kernel(x, ln_mix_g, ln_mix_b, ln_ffn_g, ln_ffn_b, ffn_w_in, ffn_w_out, mla_w_down, mla_q_norm_g,
           mla_w_uq, mla_kv_norm_g, mla_w_ukv, mla_w_o, sb_w_qkv, sb_w_o, ca_w_qkv, ca_rel_bias, ca_w_o):
    batch, seq, d = x.shape
    h = x.reshape(batch * seq, d)
    for i in range(DEPTH):
        kind = i % 3
        slot = i // 3
        if kind == 0:
            a = _mla_mixer(h, mla_w_down[slot], mla_q_norm_g[slot], mla_w_uq[slot],
                           mla_kv_norm_g[slot], mla_w_ukv[slot], batch=batch, seq=seq)
            w_o = mla_w_o[slot]
        elif kind == 1:
            qkv = _qkv_proj(h, sb_w_qkv[slot].astype(BF16))
            a = _sb_attn(qkv, batch=batch, seq=seq)
            w_o = sb_w_o[slot]
        else:
            qkv = _qkv_proj(h, ca_w_qkv[slot].astype(BF16))
            a = _ca_attn(qkv, _ca_bias(ca_rel_bias[slot]), _ca_band_mask(), batch=batch, seq=seq)
            w_o = ca_w_o[slot]
        h = _proj_ln(a, w_o.astype(BF16), h, ln_mix_g[i][None, :], ln_mix_b[i][None, :])
        h = _ffn(h, ffn_w_in[i].astype(BF16), ffn_w_out[i].astype(BF16),
                 ln_ffn_g[i][None, :], ln_ffn_b[i][None, :])
    return h.reshape(batch, seq, d)
```
